```python
import math
import jax, jax.numpy as jnp
from jax import lax
import numpy as np

D_MODEL = 1024
BATCH = 8
SEQ = 4096
DEPTH = 2
DEC_BATCH = 16
DEC_SEQ = 64
PAST_LEN = 2048

CHUNK = 64
N_MIXERS = 2
N_HGRN_LAYERS = (DEPTH + 1) // 2
N_SGU_LAYERS = DEPTH // 2
HGRN_HEADS = 8
HGRN_DK = 128
HGRN_DV = D_MODEL // HGRN_HEADS
HGRN_DF = HGRN_HEADS * HGRN_DK
HGRN_DI = HGRN_HEADS * HGRN_DV
HGRN_BLOCK = 16
NORM_EPS = 1e-5
SGU_CHUNK = 128
SGU_GROUPS = 8
SGU_WIDTH = 2 * D_MODEL
SGU_GROUP_DIM = SGU_WIDTH // SGU_GROUPS
N_EXPERTS = 32
TOP_K = 4
D_FF_EXPERT = D_MODEL
SWIGLU_LIMIT = 7.0
SWIGLU_ALPHA = 1.702
MOE_BLOCK = 256
DEEPNORM_ALPHA = (2.0 * DEPTH) ** 0.25
DEEPNORM_BETA = (8.0 * DEPTH) ** -0.25

kernel_name = "hgrn2_gmlp_moe_streaming_step"


def layer_norm(x, g, b):
    xf = x.astype(jnp.float32)
    mu = jnp.mean(xf, axis=-1, keepdims=True)
    var = jnp.mean(jnp.square(xf - mu), axis=-1, keepdims=True)
    return ((xf - mu) * lax.rsqrt(var + NORM_EPS) * g + b).astype(x.dtype)


def hgrn2_mixer(x, s0, w_in, lb, norm_w, w_out):
    bsz, length, _ = x.shape
    proj = (x @ w_in).astype(jnp.float32)
    q, fl, inp, g = jnp.split(proj, [HGRN_DF, 2 * HGRN_DF, 2 * HGRN_DF + HGRN_DI], axis=-1)
    q = jax.nn.silu(q)
    f = lb + (1.0 - lb) * jax.nn.sigmoid(fl)
    k = 1.0 - f
    logf = jnp.log(f)
    pad = (-length) % HGRN_BLOCK
    n_blk = (length + pad) // HGRN_BLOCK

    def to_blocks(t, dh):
        t = jnp.pad(t, ((0, 0), (0, pad), (0, 0)))
        t = t.reshape(bsz, n_blk, HGRN_BLOCK, HGRN_HEADS, dh)
        return t.transpose(1, 0, 3, 2, 4)

    qb = to_blocks(q, HGRN_DK)
    kb = to_blocks(k, HGRN_DK)
    vb = to_blocks(inp, HGRN_DV)
    bb = jnp.cumsum(to_blocks(logf, HGRN_DK), axis=3)
    causal = jnp.tril(jnp.ones((HGRN_BLOCK, HGRN_BLOCK), dtype=bool))

    def step(s, blk):
        qc, kc, vc, bc = blk
        o_inter = jnp.einsum('bhtk,bhkv->bhtv', qc * jnp.exp(bc), s)
        diff = bc[:, :, :, None, :] - bc[:, :, None, :, :]
        decay = jnp.exp(jnp.where(causal[None, None, :, :, None], diff, -jnp.inf))
        scores = jnp.einsum('bhtk,bhsk,bhtsk->bhts', qc, kc, decay)
        o_intra = jnp.einsum('bhts,bhsv->bhtv', scores, vc)
        b_last = bc[:, :, -1]
        s_new = jnp.exp(b_last)[..., None] * s + jnp.einsum(
            'bhsk,bhsv->bhkv', kc * jnp.exp(b_last[:, :, None, :] - bc), vc)
        return s_new, o_inter + o_intra

    s_fin, ob = lax.scan(step, s0.astype(jnp.float32), (qb, kb, vb, bb))
    o = ob.transpose(1, 0, 3, 2, 4).reshape(bsz, n_blk * HGRN_BLOCK, HGRN_HEADS, HGRN_DV)[:, :length]
    o = o * lax.rsqrt(jnp.mean(o * o, axis=-1, keepdims=True) + NORM_EPS) * norm_w
    o = o.reshape(bsz, length, HGRN_DI) * jax.nn.silu(g)
    return o.astype(x.dtype) @ w_out, s_fin.astype(s0.dtype)


def sgu_mixer(x, w_in, ln_g, ln_b, w_s, b_s, w_out):
    bsz, length, _ = x.shape
    z = jax.nn.gelu(x @ w_in, approximate=False)
    u, v = jnp.split(z, 2, axis=-1)
    v = layer_norm(v, ln_g, ln_b)
    pad = (-length) % SGU_CHUNK
    n_ch = (length + pad) // SGU_CHUNK
    vc = jnp.pad(v, ((0, 0), (0, pad), (0, 0))).reshape(bsz, n_ch, SGU_CHUNK, SGU_GROUPS, SGU_GROUP_DIM)
    tril = jnp.tril(jnp.ones((SGU_CHUNK, SGU_CHUNK), dtype=bool))
    w_causal = jnp.where(tril[None], w_s, 0.0)
    mixed = jnp.einsum('gts,bcsgd->bctgd', w_causal, vc) + b_s.T[None, None, :, :, None]
    mixed = mixed.reshape(bsz, n_ch * SGU_CHUNK, SGU_WIDTH)[:, :length]
    return (u * mixed) @ w_out, v


def moe_ffn(x, w_router, b_router, w_up, b_up, w_down, b_down):
    n_tok = x.shape[0]
    logits = (x @ w_router).astype(jnp.float32) + b_router.astype(jnp.float32)
    top_logit, top_idx = lax.top_k(logits, TOP_K)
    gates = jax.nn.softmax(top_logit, axis=-1)
    n_slots = n_tok * TOP_K
    slot_e = top_idx.reshape(n_slots)
    order = jnp.argsort(slot_e)
    sorted_e = slot_e[order]
    counts = jnp.bincount(slot_e, length=N_EXPERTS)
    padded = (counts + MOE_BLOCK - 1) // MOE_BLOCK * MOE_BLOCK
    pad_end = jnp.cumsum(padded)
    pad_start = pad_end - padded
    grp_start = jnp.cumsum(counts) - counts
    dest = pad_start[sorted_e] + jnp.arange(n_slots) - grp_start[sorted_e]
    n_blocks = -(-(n_slots + N_EXPERTS * (MOE_BLOCK - 1)) // MOE_BLOCK)
    buf = jnp.zeros((n_blocks * MOE_BLOCK, D_MODEL), x.dtype).at[dest].set(x[order // TOP_K])
    block_e = jnp.minimum(
        jnp.searchsorted(pad_end, jnp.arange(n_blocks) * MOE_BLOCK, side='right'), N_EXPERTS - 1)

    def expert_block(args):
        xb, e = args
        h = xb @ w_up[e] + b_up[e]
        glu, lin = jnp.split(h, 2, axis=-1)
        glu = jnp.minimum(glu, SWIGLU_LIMIT)
        lin = jnp.clip(lin, -SWIGLU_LIMIT, SWIGLU_LIMIT)
        act = glu * jax.nn.sigmoid(SWIGLU_ALPHA * glu) * (lin + 1.0)
        return act @ w_down[e] + b_down[e]

    out_buf = lax.map(expert_block, (buf.reshape(n_blocks, MOE_BLOCK, D_MODEL), block_e))
    out_sorted = out_buf.reshape(n_blocks * MOE_BLOCK, D_MODEL)[dest]
    out_slots = jnp.zeros_like(out_sorted).at[order].set(out_sorted).reshape(n_tok, TOP_K, D_MODEL)
    return jnp.einsum('tkd,tk->td', out_slots, gates.astype(out_slots.dtype))


def run_trunk(x, hgrn_s0, hgrn_w_in, hgrn_lb_logits, hgrn_norm_w, hgrn_w_out,
              sgu_w_in, sgu_ln_g, sgu_ln_b, sgu_w_s, sgu_b_s, sgu_w_out,
              ln_mix_g, ln_mix_b, ln_ffn_g, ln_ffn_b,
              moe_w_router, moe_b_router, moe_w_up, moe_b_up, moe_w_down, moe_b_down):
    lb_all = jnp.cumsum(jax.nn.softmax(hgrn_lb_logits.astype(jnp.float32), axis=0), axis=0)
    hgrn_states = []
    sgu_rows = []
    for layer in range(DEPTH):
        j = layer // N_MIXERS
        if layer % N_MIXERS == 0:
            h, s = hgrn2_mixer(x, hgrn_s0[j], hgrn_w_in[j], lb_all[layer], hgrn_norm_w[j], hgrn_w_out[j])
            hgrn_states.append(s)
        else:
            h, v = sgu_mixer(x, sgu_w_in[j], sgu_ln_g[j], sgu_ln_b[j], sgu_w_s[j], sgu_b_s[j], sgu_w_out[j])
            sgu_rows.append(v)
        x = layer_norm(DEEPNORM_ALPHA * x + h, ln_mix_g[layer], ln_mix_b[layer])
        ff = moe_ffn(x.reshape(-1, D_MODEL), moe_w_router[layer], moe_b_router[layer], moe_w_up[layer],
                     moe_b_up[layer], moe_w_down[layer], moe_b_down[layer]).reshape(x.shape)
        x = layer_norm(DEEPNORM_ALPHA * x + ff, ln_ffn_g[layer], ln_ffn_b[layer])
    return x, jnp.stack(hgrn_states), jnp.stack(sgu_rows)


def _normal(k, shape, scale):
    return jax.random.normal(k, shape, jnp.float32) * scale


def setup_inputs(seed: int = 0) -> dict:
    key = jax.random.key(seed)
    ks = jax.random.split(key, 24)
    return {
        "x_prompt": _normal(ks[0], (BATCH, SEQ, D_MODEL), 1.0),
        "x_sample": _normal(ks[1], (DEC_BATCH, DEC_SEQ, D_MODEL), 1.0),
        "state_hgrn": _normal(ks[2], (N_HGRN_LAYERS, DEC_BATCH, HGRN_HEADS, HGRN_DK, HGRN_DV), 0.5),
        "hgrn_w_in": _normal(ks[3], (N_HGRN_LAYERS, D_MODEL, 2 * HGRN_DF + 2 * HGRN_DI), D_MODEL ** -0.5),
        "hgrn_lb_logits": _normal(ks[4], (DEPTH + 1, HGRN_DF), 0.5),
        "hgrn_norm_w": 1.0 + _normal(ks[5], (N_HGRN_LAYERS, HGRN_DV), 0.02),
        "hgrn_w_out": _normal(ks[6], (N_HGRN_LAYERS, HGRN_DI, D_MODEL), HGRN_DI ** -0.5 * DEEPNORM_BETA),
        "sgu_w_in": _normal(ks[7], (N_SGU_LAYERS, D_MODEL, 2 * SGU_WIDTH), D_MODEL ** -0.5),
        "sgu_ln_g": 1.0 + _normal(ks[8], (N_SGU_LAYERS, SGU_WIDTH), 0.02),
        "sgu_ln_b": _normal(ks[9], (N_SGU_LAYERS, SGU_WIDTH), 0.02),
        "sgu_w_s": _normal(ks[10], (N_SGU_LAYERS, SGU_GROUPS, SGU_CHUNK, SGU_CHUNK), SGU_CHUNK ** -0.5),
        "sgu_b_s": 1.0 + _normal(ks[11], (N_SGU_LAYERS, SGU_GROUPS, SGU_CHUNK), 0.02),
        "sgu_w_out": _normal(ks[12], (N_SGU_LAYERS, SGU_WIDTH, D_MODEL), SGU_WIDTH ** -0.5 * DEEPNORM_BETA),
        "ln_mix_g": 1.0 + _normal(ks[13], (DEPTH, D_MODEL), 0.02),
        "ln_mix_b": _normal(ks[14], (DEPTH, D_MODEL), 0.02),
        "ln_ffn_g": 1.0 + _normal(ks[15], (DEPTH, D_MODEL), 0.02),
        "ln_ffn_b": _normal(ks[16], (DEPTH, D_MODEL), 0.02),
        "moe_w_router": _normal(ks[17], (DEPTH, D_MODEL, N_EXPERTS), D_MODEL ** -0.5),
        "moe_b_router": _normal(ks[18], (DEPTH, N_EXPERTS), 0.01),
        "moe_w_up": _normal(ks[19], (DEPTH, N_EXPERTS, D_MODEL, 2 * D_FF_EXPERT), D_MODEL ** -0.5),
        "moe_b_up": _normal(ks[20], (DEPTH, N_EXPERTS, 2 * D_FF_EXPERT), 0.02),
        "moe_w_down": _normal(ks[21], (DEPTH, N_EXPERTS, D_FF_EXPERT, D_MODEL), D_FF_EXPERT ** -0.5 * DEEPNORM_BETA),
        "moe_b_down": _normal(ks[22], (DEPTH, N_EXPERTS, D_MODEL), 0.02),
    }


def reference(x_prompt, x_sample, state_hgrn, hgrn_w_in, hgrn_lb_logits, hgrn_norm_w, hgrn_w_out,
              sgu_w_in, sgu_ln_g, sgu_ln_b, sgu_w_s, sgu_b_s, sgu_w_out,
              ln_mix_g, ln_mix_b, ln_ffn_g, ln_ffn_b,
              moe_w_router, moe_b_router, moe_w_up, moe_b_up, moe_w_down, moe_b_down):
    hgrn_zero = jnp.zeros((N_HGRN_LAYERS, x_prompt.shape[0], HGRN_HEADS, HGRN_DK, HGRN_DV), x_prompt.dtype)
    y_prompt, hgrn_state_prompt, _ = run_trunk(
        x_prompt, hgrn_zero, hgrn_w_in, hgrn_lb_logits, hgrn_norm_w, hgrn_w_out,
        sgu_w_in, sgu_ln_g, sgu_ln_b, sgu_w_s, sgu_b_s, sgu_w_out,
        ln_mix_g, ln_mix_b, ln_ffn_g, ln_ffn_b,
        moe_w_router, moe_b_router, moe_w_up, moe_b_up, moe_w_down, moe_b_down)
    y_sample, hgrn_state_sample, sgu_v_sample = run_trunk(
        x_sample, state_hgrn, hgrn_w_in, hgrn_lb_logits, hgrn_norm_w, hgrn_w_out,
        sgu_w_in, sgu_ln_g, sgu_ln_b, sgu_w_s, sgu_b_s, sgu_w_out,
        ln_mix_g, ln_mix_b, ln_ffn_g, ln_ffn_b,
        moe_w_router, moe_b_router, moe_w_up, moe_b_up, moe_w_down, moe_b_down)
    return (y_prompt, y_sample, hgrn_state_prompt, hgrn_state_sample, sgu_v_sample)
```

```python
import functools

import jax
import jax.numpy as jnp
from jax import lax
from jax.experimental import pallas as pl
from jax.experimental.pallas import tpu as pltpu

F32 = jnp.float32
BF16 = jnp.bfloat16
I32 = jnp.int32

D_MODEL = 1024
DEPTH = 2
HEADS = 8
DK = 128
DV = 128
NORM_EPS = 1e-5
SGU_CHUNK = 128
SGU_GROUPS = 8
SGU_WIDTH = 2 * D_MODEL
SGU_GROUP_DIM = SGU_WIDTH // SGU_GROUPS
N_EXPERTS = 32
TOP_K = 4
D_FF = D_MODEL
SWIGLU_LIMIT = 7.0
SWIGLU_ALPHA = 1.702
ALPHA = (2.0 * DEPTH) ** 0.25

LANES = 128
SUBLANES = 8
ROW_TILES = D_MODEL // LANES
RCHUNK = 64
MIX_ROWS = 256
ROUTE_ROWS = 512
MOE_ROWS = 256
VMEM_LIMIT = 56 * 1024 * 1024


def _dot(a, b):
    return jnp.dot(a, b, preferred_element_type=F32)


def _dot_nt(a, b):
    return lax.dot_general(a, b, (((1,), (1,)), ((), ())), preferred_element_type=F32)


def _dot_tn(a, b):
    return lax.dot_general(a, b, (((0,), (0,)), ((), ())), preferred_element_type=F32)


def _layer_norm(z, g, b):
    mu = jnp.mean(z, axis=-1, keepdims=True)
    zc = z - mu
    var = jnp.mean(zc * zc, axis=-1, keepdims=True)
    return zc * lax.rsqrt(var + NORM_EPS) * g + b


def _gelu(z):
    return 0.5 * z * (1.0 + lax.erf(z * (2.0 ** -0.5)))


def _params(n_axes):
    return pltpu.CompilerParams(dimension_semantics=("arbitrary",) * n_axes,
                                vmem_limit_bytes=VMEM_LIMIT)


def _hgrn_kernel(x_ref, s0_ref, win_ref, lb_ref, nw_ref, wout_ref, lg_ref, lbias_ref,
                 o_ref, sout_ref, q_s, lf_s, k_s, v_s, g_s, o_s, st_s, *, cx):
    j = pl.program_id(1)

    @pl.when(j == 0)
    def _():
        for h in range(HEADS):
            st_s[h] = s0_ref[0, h].T

    x = x_ref[...]
    xb = x.astype(BF16)
    hd = HEADS * DK
    q = _dot(xb, win_ref[:, 0:hd])
    q_s[...] = q * jax.nn.sigmoid(q)
    lb = lb_ref[...]
    f = lb + (1.0 - lb) * jax.nn.sigmoid(_dot(xb, win_ref[:, hd:2 * hd]))
    lf_s[...] = jnp.log(f)
    k_s[...] = 1.0 - f
    v_s[...] = _dot(xb, win_ref[:, 2 * hd:3 * hd])
    g = _dot(xb, win_ref[:, 3 * hd:4 * hd])
    g_s[...] = g * jax.nn.sigmoid(g)

    row = lax.broadcasted_iota(I32, (RCHUNK, DK), 0)
    ti = lax.broadcasted_iota(I32, (RCHUNK, RCHUNK), 0)
    si = lax.broadcasted_iota(I32, (RCHUNK, RCHUNK), 1)
    nw = nw_ref[...]

    def head_chunk(idx, carry):
        c = idx // HEADS
        h = idx - c * HEADS
        rows = pl.ds(pl.multiple_of(c * RCHUNK, RCHUNK), RCHUNK)
        lanes = pl.ds(pl.multiple_of(h * DK, DK), DK)
        q = q_s[rows, lanes]
        k = k_s[rows, lanes]
        v = v_s[rows, lanes]
        gc = lf_s[rows, lanes]
        for sh in (1, 2, 4, 8, 16, 32):
            gc = gc + jnp.where(row >= sh, pltpu.roll(gc, sh, axis=0), 0.0)
        glast = gc[RCHUNK - 1:RCHUNK, :]
        st = st_s[h]
        vb = v.astype(BF16)
        o = _dot_nt((q * jnp.exp(gc)).astype(BF16), st.astype(BF16))
        kd = (k * jnp.exp(glast - gc)).astype(BF16)
        st_s[h] = st * jnp.exp(glast) + _dot_tn(vb, kd)

        a = jnp.where(ti == si, _dot_nt(q.astype(BF16), k.astype(BF16)), 0.0)
        rolled = {}

        def shifted(d):
            d = d % RCHUNK
            if d not in rolled:
                rolled[d] = pltpu.roll(gc, d, axis=0)
            return rolled[d]

        for bs in (64, 32, 16, 8, 4, 2):
            half = bs // 2
            rm = row & (bs - 1)
            right = rm >= half
            if bs >= 2 * SUBLANES:
                ref = jnp.concatenate(
                    [jnp.broadcast_to(gc[b * bs + half - 1:b * bs + half, :], (bs, DK))
                     for b in range(RCHUNK // bs)], axis=0)
            else:
                ref = gc
                for d in range(1, half + 1):
                    ref = jnp.where(rm == half - 1 + d, shifted(d), ref)
                for d in range(1, half):
                    ref = jnp.where(rm == half - 1 - d, shifted(-d), ref)
            fac = jnp.exp(jnp.where(right, gc - ref, ref - gc))
            ql = jnp.where(right, q * fac, 0.0).astype(BF16)
            kl = jnp.where(right, 0.0, k * fac).astype(BF16)
            shift = bs.bit_length() - 1
            a = a + jnp.where((ti >> shift) == (si >> shift), _dot_nt(ql, kl), 0.0)
        o = o + _dot(a.astype(BF16), vb)
        o = o * lax.rsqrt(jnp.mean(o * o, axis=-1, keepdims=True) + NORM_EPS) * nw
        o_s[rows, lanes] = (o * g_s[rows, lanes]).astype(BF16)
        return carry

    lax.fori_loop(0, (cx // RCHUNK) * HEADS, head_chunk, 0)

    y = _dot(o_s[...], wout_ref[...])
    o_ref[...] = _layer_norm(ALPHA * x + y, lg_ref[...], lbias_ref[...])

    @pl.when(j == pl.num_programs(1) - 1)
    def _():
        for h in range(HEADS):
            sout_ref[0, h] = st_s[h].T


def _hgrn_layer(x_flat, row0, bsz, length, s0, w_in, lb, norm_w, w_out, ln_g, ln_b):
    t_all = x_flat.shape[0]
    cx = min(length, MIX_ROWS)
    assert length % cx == 0 and cx % RCHUNK == 0 and row0 % cx == 0
    nj = length // cx
    blk0 = row0 // cx
    const = lambda b, j: (0, 0)
    in_specs = [
        pl.BlockSpec((cx, D_MODEL), lambda b, j: (blk0 + b * nj + j, 0)),
        pl.BlockSpec((1, HEADS, DK, DV), lambda b, j: (b, 0, 0, 0)),
        pl.BlockSpec(w_in.shape, const),
        pl.BlockSpec((1, HEADS * DK), const),
        pl.BlockSpec((1, DV), const),
        pl.BlockSpec(w_out.shape, const),
        pl.BlockSpec((1, D_MODEL), const),
        pl.BlockSpec((1, D_MODEL), const),
    ]
    args = [x_flat, s0, w_in, lb, norm_w, w_out, ln_g, ln_b]
    return pl.pallas_call(
        functools.partial(_hgrn_kernel, cx=cx),
        grid=(bsz, nj),
        in_specs=in_specs,
        out_specs=[
            pl.BlockSpec((cx, D_MODEL), lambda b, j: (blk0 + b * nj + j, 0)),
            pl.BlockSpec((1, HEADS, DK, DV), lambda b, j: (b, 0, 0, 0)),
        ],
        out_shape=[
            jax.ShapeDtypeStruct((t_all, D_MODEL), F32),
            jax.ShapeDtypeStruct((bsz, HEADS, DK, DV), F32),
        ],
        scratch_shapes=[pltpu.VMEM((cx, HEADS * DK), F32)] * 5
        + [pltpu.VMEM((cx, HEADS * DV), BF16), pltpu.VMEM((HEADS, DV, DK), F32)],
        input_output_aliases={0: 0},
        compiler_params=_params(2),
        name="hgrn_layer",
    )(*args)


def _sgu_kernel(*refs, cx, chunk, with_v):
    (x_ref, win_ref, vg_ref, vb_ref, ws_ref, bs_ref, wout_ref, lg_ref, lbias_ref) = refs[:9]
    o_ref = refs[9]
    v_ref = refs[10] if with_v else None
    x = x_ref[...]
    xb = x.astype(BF16)
    u = _gelu(_dot(xb, win_ref[:, 0:SGU_WIDTH]))
    v = _gelu(_dot(xb, win_ref[:, SGU_WIDTH:2 * SGU_WIDTH]))
    v = _layer_norm(v, vg_ref[...], vb_ref[...])
    if with_v:
        v_ref[...] = v
    vbf = v.astype(BF16)
    ti = lax.broadcasted_iota(I32, (chunk, chunk), 0)
    si = lax.broadcasted_iota(I32, (chunk, chunk), 1)
    cols = []
    for g in range(SGU_GROUPS):
        wc = jnp.where(ti >= si, ws_ref[g, 0:chunk, 0:chunk], 0.0).astype(BF16)
        bias = bs_ref[g, 0:chunk, :]
        lanes = slice(g * SGU_GROUP_DIM, (g + 1) * SGU_GROUP_DIM)
        cols.append(jnp.concatenate(
            [_dot(wc, vbf[c * chunk:(c + 1) * chunk, lanes]) + bias for c in range(cx // chunk)],
            axis=0))
    mixed = jnp.concatenate(cols, axis=1)
    y = _dot((u * mixed).astype(BF16), wout_ref[...])
    o_ref[...] = _layer_norm(ALPHA * x + y, lg_ref[...], lbias_ref[...])


def _sgu_layer(x_flat, row0, bsz, length, w_in, vg, vb, w_s, b_s, w_out, ln_g, ln_b, with_v):
    t_all = x_flat.shape[0]
    chunk = min(length, SGU_CHUNK)
    cx = min(length, MIX_ROWS)
    assert length % cx == 0 and cx % chunk == 0 and row0 % cx == 0
    n_rows = bsz * length
    blk0 = row0 // cx
    const2 = lambda i: (0, 0)
    const3 = lambda i: (0, 0, 0)
    in_specs = [
        pl.BlockSpec((cx, D_MODEL), lambda i: (blk0 + i, 0)),
        pl.BlockSpec(w_in.shape, const2),
        pl.BlockSpec((1, SGU_WIDTH), const2),
        pl.BlockSpec((1, SGU_WIDTH), const2),
        pl.BlockSpec(w_s.shape, const3),
        pl.BlockSpec(b_s.shape, const3),
        pl.BlockSpec(w_out.shape, const2),
        pl.BlockSpec((1, D_MODEL), const2),
        pl.BlockSpec((1, D_MODEL), const2),
    ]
    args = [x_flat, w_in, vg, vb, w_s, b_s, w_out, ln_g, ln_b]
    out_specs = [pl.BlockSpec((cx, D_MODEL), lambda i: (blk0 + i, 0))]
    out_shape = [jax.ShapeDtypeStruct((t_all, D_MODEL), F32)]
    if with_v:
        out_specs.append(pl.BlockSpec((cx, SGU_WIDTH), lambda i: (i, 0)))
        out_shape.append(jax.ShapeDtypeStruct((n_rows, SGU_WIDTH), F32))
    return pl.pallas_call(
        functools.partial(_sgu_kernel, cx=cx, chunk=chunk, with_v=with_v),
        grid=(n_rows // cx,),
        in_specs=in_specs,
        out_specs=out_specs,
        out_shape=out_shape,
        input_output_aliases={0: 0},
        compiler_params=_params(1),
        name="sgu_layer",
    )(*args)


def _router_kernel(x_ref, wt_ref, b_ref, idx_ref, gate_ref, rank_ref, cnt_ref, carry_s):
    i = pl.program_id(0)

    @pl.when(i == 0)
    def _():
        carry_s[...] = jnp.zeros_like(carry_s)

    tb = x_ref.shape[0]
    logits = lax.dot_general(wt_ref[...], x_ref[...], (((1,), (1,)), ((), ())),
                             preferred_element_type=F32,
                             precision=lax.Precision.HIGHEST) + b_ref[...]
    erow = lax.broadcasted_iota(I32, (N_EXPERTS, tb), 0)
    work = logits
    tops, hots = [], []
    for kk in range(TOP_K):
        m = jnp.max(work, axis=0, keepdims=True)
        idx = jnp.min(jnp.where(work == m, erow, N_EXPERTS), axis=0, keepdims=True)
        hot = erow == idx
        idx_ref[kk:kk + 1, :] = idx
        tops.append(m)
        hots.append(hot)
        work = jnp.where(hot, -jnp.inf, work)
    exps = [jnp.exp(m - tops[0]) for m in tops]
    denom = exps[0] + exps[1] + exps[2] + exps[3]
    for kk in range(TOP_K):
        gate_ref[kk:kk + 1, :] = exps[kk] / denom
    sel = jnp.zeros((N_EXPERTS, tb), F32)
    for hot in hots:
        sel = sel + hot.astype(F32)
    t0 = lax.broadcasted_iota(I32, (tb, tb), 0)
    t1 = lax.broadcasted_iota(I32, (tb, tb), 1)
    before = (t0 < t1).astype(BF16)
    prefix = _dot(sel.astype(BF16), before) + carry_s[...]
    for kk in range(TOP_K):
        r = jnp.sum(jnp.where(hots[kk], prefix, 0.0), axis=0, keepdims=True)
        rank_ref[kk:kk + 1, :] = r.astype(I32)
    carry_s[...] = carry_s[...] + jnp.sum(sel, axis=1, keepdims=True)

    @pl.when(i == pl.num_programs(0) - 1)
    def _():
        cnt_ref[...] = carry_s[...].astype(I32)


def _router(x_flat, w_router_t, b_router):
    t_all = x_flat.shape[0]
    tb = ROUTE_ROWS
    assert t_all % tb == 0
    const = lambda i: (0, 0)
    return pl.pallas_call(
        _router_kernel,
        grid=(t_all // tb,),
        in_specs=[
            pl.BlockSpec((tb, D_MODEL), lambda i: (i, 0)),
            pl.BlockSpec((N_EXPERTS, D_MODEL), const),
            pl.BlockSpec((N_EXPERTS, 1), const),
        ],
        out_specs=[
            pl.BlockSpec((TOP_K, tb), lambda i: (0, i)),
            pl.BlockSpec((TOP_K, tb), lambda i: (0, i)),
            pl.BlockSpec((TOP_K, tb), lambda i: (0, i)),
            pl.BlockSpec((N_EXPERTS, 1), const),
        ],
        out_shape=[
            jax.ShapeDtypeStruct((TOP_K, t_all), I32),
            jax.ShapeDtypeStruct((TOP_K, t_all), F32),
            jax.ShapeDtypeStruct((TOP_K, t_all), I32),
            jax.ShapeDtypeStruct((N_EXPERTS, 1), I32),
        ],
        scratch_shapes=[pltpu.VMEM((N_EXPERTS, 1), F32)],
        compiler_params=_params(1),
        name="moe_router",
    )(x_flat, w_router_t, b_router)


def _row_copy(src, src_row, dst, dst_row, sem):
    return pltpu.make_async_copy(
        src.at[pl.ds(pl.multiple_of(src_row * ROW_TILES, ROW_TILES), ROW_TILES)],
        dst.at[pl.ds(pl.multiple_of(dst_row * ROW_TILES, ROW_TILES), ROW_TILES)],
        sem)


def _dispatch_kernel(dest_ref, x_ref, buf_in_ref, buf_ref, stage, sem):
    del buf_in_ref
    tb = x_ref.shape[0]
    for c in range(ROW_TILES):
        stage[pl.ds(c, tb, stride=ROW_TILES), :] = x_ref[:, c * LANES:(c + 1) * LANES]

    def issue(t, carry):
        for kk in range(TOP_K):
            _row_copy(stage, t, buf_ref, dest_ref[0, 0, kk * tb + t], sem).start()
        return carry

    lax.fori_loop(0, tb, issue, 0)

    def drain(t, carry):
        for kk in range(TOP_K):
            _row_copy(stage, 0, buf_ref, 0, sem).wait()
        return carry

    lax.fori_loop(0, tb, drain, 0)


def _dispatch(x_flat, dest_blocks, buf_zero):
    t_all = x_flat.shape[0]
    tb = MOE_ROWS
    return pl.pallas_call(
        _dispatch_kernel,
        grid=(t_all // tb,),
        in_specs=[
            pl.BlockSpec((1, 1, TOP_K * tb), lambda i: (i, 0, 0), memory_space=pltpu.SMEM),
            pl.BlockSpec((tb, D_MODEL), lambda i: (i, 0)),
            pl.BlockSpec(memory_space=pl.ANY),
        ],
        out_specs=pl.BlockSpec(memory_space=pl.ANY),
        out_shape=jax.ShapeDtypeStruct(buf_zero.shape, F32),
        scratch_shapes=[pltpu.VMEM((tb * ROW_TILES, LANES), F32), pltpu.SemaphoreType.DMA(())],
        input_output_aliases={2: 0},
        compiler_params=_params(1),
        name="moe_dispatch",
    )(dest_blocks, x_flat, buf_zero)


def _expert_kernel(be_ref, nused_ref, x_ref, wup_ref, bup_ref, wdn_ref, bdn_ref, o_ref):
    i = pl.program_id(0)
    tm = x_ref.shape[0] // ROW_TILES

    @pl.when(i < nused_ref[0])
    def _():
        xg = jnp.concatenate(
            [x_ref[pl.ds(c, tm, stride=ROW_TILES), :] for c in range(ROW_TILES)], axis=1)
        h = _dot(xg.astype(BF16), wup_ref[0]) + bup_ref[0]
        glu = jnp.minimum(h[:, 0:D_FF], SWIGLU_LIMIT)
        lin = jnp.clip(h[:, D_FF:2 * D_FF], -SWIGLU_LIMIT, SWIGLU_LIMIT)
        act = glu * jax.nn.sigmoid(SWIGLU_ALPHA * glu) * (lin + 1.0)
        y = _dot(act.astype(BF16), wdn_ref[0]) + bdn_ref[0]
        for c in range(ROW_TILES):
            o_ref[pl.ds(c, tm, stride=ROW_TILES), :] = y[:, c * LANES:(c + 1) * LANES]

    @pl.when(i >= nused_ref[0])
    def _():
        o_ref[...] = jnp.zeros_like(o_ref)


def _experts(buf, block_e, nused, w_up, b_up, w_down, b_down):
    tm = MOE_ROWS
    n_blocks = buf.shape[0] // (tm * ROW_TILES)
    return pl.pallas_call(
        _expert_kernel,
        grid_spec=pltpu.PrefetchScalarGridSpec(
            num_scalar_prefetch=2,
            grid=(n_blocks,),
            in_specs=[
                pl.BlockSpec((tm * ROW_TILES, LANES), lambda i, be, nu: (i, 0)),
                pl.BlockSpec((1, D_MODEL, 2 * D_FF), lambda i, be, nu: (be[i], 0, 0)),
                pl.BlockSpec((1, 1, 2 * D_FF), lambda i, be, nu: (be[i], 0, 0)),
                pl.BlockSpec((1, D_FF, D_MODEL), lambda i, be, nu: (be[i], 0, 0)),
                pl.BlockSpec((1, 1, D_MODEL), lambda i, be, nu: (be[i], 0, 0)),
            ],
            out_specs=pl.BlockSpec((tm * ROW_TILES, LANES), lambda i, be, nu: (i, 0)),
        ),
        out_shape=jax.ShapeDtypeStruct(buf.shape, F32),
        compiler_params=_params(1),
        name="moe_experts",
    )(block_e, nused, buf, w_up, b_up, w_down, b_down)


def _combine_kernel(dest_ref, x_ref, gate_ref, ybuf_ref, lg_ref, lbias_ref, o_ref, rows, sem):
    tb = x_ref.shape[0]

    def issue(t, carry):
        for kk in range(TOP_K):
            _row_copy(ybuf_ref, dest_ref[0, 0, kk * tb + t], rows, kk * tb + t, sem).start()
        return carry

    lax.fori_loop(0, tb, issue, 0)

    def drain(t, carry):
        for kk in range(TOP_K):
            _row_copy(ybuf_ref, 0, rows, 0, sem).wait()
        return carry

    lax.fori_loop(0, tb, drain, 0)

    gates = gate_ref[...]
    ff = jnp.zeros((tb, D_MODEL), F32)
    for kk in range(TOP_K):
        yk = jnp.concatenate(
            [rows[pl.ds(kk * tb * ROW_TILES + c, tb, stride=ROW_TILES), :]
             for c in range(ROW_TILES)], axis=1)
        ff = ff + yk * gates[:, kk:kk + 1]
    o_ref[...] = _layer_norm(ALPHA * x_ref[...] + ff, lg_ref[...], lbias_ref[...])


def _combine(x_flat, dest_blocks, gates_t, ybuf, ln_g, ln_b):
    t_all = x_flat.shape[0]
    tb = MOE_ROWS
    const = lambda i: (0, 0)
    return pl.pallas_call(
        _combine_kernel,
        grid=(t_all // tb,),
        in_specs=[
            pl.BlockSpec((1, 1, TOP_K * tb), lambda i: (i, 0, 0), memory_space=pltpu.SMEM),
            pl.BlockSpec((tb, D_MODEL), lambda i: (i, 0)),
            pl.BlockSpec((tb, TOP_K), lambda i: (i, 0)),
            pl.BlockSpec(memory_space=pl.ANY),
            pl.BlockSpec((1, D_MODEL), const),
            pl.BlockSpec((1, D_MODEL), const),
        ],
        out_specs=pl.BlockSpec((tb, D_MODEL), lambda i: (i, 0)),
        out_shape=jax.ShapeDtypeStruct((t_all, D_MODEL), F32),
        scratch_shapes=[pltpu.VMEM((TOP_K * tb * ROW_TILES, LANES), F32),
                        pltpu.SemaphoreType.DMA(())],
        compiler_params=_params(1),
        name="moe_combine",
    )(dest_blocks, x_flat, gates_t, ybuf, ln_g, ln_b)


def _moe_layer(x_flat, w_router, b_router, w_up, b_up, w_down, b_down, ln_g, ln_b):
    t_all = x_flat.shape[0]
    tm = MOE_ROWS
    idx, gates, rank, counts = _router(x_flat, w_router.T, b_router.reshape(N_EXPERTS, 1))
    counts = counts.reshape(N_EXPERTS)
    padded = (counts + tm - 1) // tm * tm
    pad_end = jnp.cumsum(padded)
    pad_start = pad_end - padded
    hot = idx[:, :, None] == jnp.arange(N_EXPERTS, dtype=I32)[None, None, :]
    dest = jnp.sum(jnp.where(hot, pad_start[None, None, :], 0), axis=-1) + rank
    n_blocks = -(-(t_all * TOP_K + N_EXPERTS * (tm - 1)) // tm)
    block_e = jnp.minimum(
        jnp.sum(pad_end[None, :] <= (jnp.arange(n_blocks, dtype=I32) * tm)[:, None], axis=1),
        N_EXPERTS - 1).astype(I32)
    nused = (pad_end[-1:] // tm).astype(I32)
    nb = t_all // tm
    dest_blocks = dest.reshape(TOP_K, nb, tm).transpose(1, 0, 2).reshape(nb, 1, TOP_K * tm)
    buf = _dispatch(x_flat, dest_blocks, jnp.zeros((n_blocks * tm * ROW_TILES, LANES), F32))
    ybuf = _experts(buf, block_e, nused, w_up, b_up.reshape(N_EXPERTS, 1, 2 * D_FF),
                    w_down, b_down.reshape(N_EXPERTS, 1, D_MODEL))
    return _combine(x_flat, dest_blocks, gates.T, ybuf, ln_g, ln_b)


def kernel(x_prompt, x_sample, state_hgrn, hgrn_w_in, hgrn_lb_logits, hgrn_norm_w, hgrn_w_out,
           sgu_w_in, sgu_ln_g, sgu_ln_b, sgu_w_s, sgu_b_s, sgu_w_out,
           ln_mix_g, ln_mix_b, ln_ffn_g, ln_ffn_b,
           moe_w_router, moe_b_router, moe_w_up, moe_b_up, moe_w_down, moe_b_down):
    bp, lp, _ = x_prompt.shape
    bs, ls, _ = x_sample.shape
    tp, ts = bp * lp, bs * ls
    x = jnp.concatenate([x_prompt.reshape(tp, D_MODEL), x_sample.reshape(ts, D_MODEL)], axis=0)
    lb_all = jnp.cumsum(jax.nn.softmax(hgrn_lb_logits.astype(F32), axis=0), axis=0)
    row = lambda a: a.reshape(1, -1)

    w_in = hgrn_w_in[0].astype(BF16)
    w_out = hgrn_w_out[0].astype(BF16)
    common = (w_in, row(lb_all[0]), row(hgrn_norm_w[0]), w_out, row(ln_mix_g[0]), row(ln_mix_b[0]))
    zero_state = jnp.zeros((bp, HEADS, DK, DV), F32)
    x1, st_prompt = _hgrn_layer(x, 0, bp, lp, zero_state, *common)
    x1, st_sample = _hgrn_layer(x1, tp, bs, ls, state_hgrn[0], *common)
    x2 = _moe_layer(x1, moe_w_router[0], moe_b_router[0], moe_w_up[0].astype(BF16), moe_b_up[0],
                    moe_w_down[0].astype(BF16), moe_b_down[0], row(ln_ffn_g[0]), row(ln_ffn_b[0]))

    common = (sgu_w_in[0].astype(BF16), row(sgu_ln_g[0]), row(sgu_ln_b[0]), sgu_w_s[0],
              sgu_b_s[0].reshape(SGU_GROUPS, SGU_CHUNK, 1), sgu_w_out[0].astype(BF16),
              row(ln_mix_g[1]), row(ln_mix_b[1]))
    (x3,) = _sgu_layer(x2, 0, bp, lp, *common, with_v=False)
    x3, v_sample = _sgu_layer(x3, tp, bs, ls, *common, with_v=True)
    x4 = _moe_layer(x3, moe_w_router[1], moe_b_router[1], moe_w_up[1].astype(BF16), moe_b_up[1],
                    moe_w_down[1].astype(BF16), moe_b_down[1], row(ln_ffn_g[1]), row(ln_ffn_b[1]))

    return (x4[:tp].reshape(bp, lp, D_MODEL), x4[tp:].reshape(bs, ls, D_MODEL),
            st_prompt[None], st_sample[None], v_sample.reshape(1, bs, ls, SGU_WIDTH))
```

```python
import functools

import jax
import jax.numpy as jnp
from jax import lax
from jax.experimental import pallas as pl
from jax.experimental.pallas import tpu as pltpu

F32 = jnp.float32
BF16 = jnp.bfloat16
I32 = jnp.int32

D_MODEL = 1024
DEPTH = 2
HEADS = 8
DK = 128
DV = 128
NORM_EPS = 1e-5
SGU_CHUNK = 128
SGU_GROUPS = 8
SGU_WIDTH = 2 * D_MODEL
SGU_GROUP_DIM = SGU_WIDTH // SGU_GROUPS
N_EXPERTS = 32
TOP_K = 4
D_FF = D_MODEL
SWIGLU_LIMIT = 7.0
SWIGLU_ALPHA = 1.702
ALPHA = (2.0 * DEPTH) ** 0.25

LANES = 128
SUBLANES = 8
ROW_TILES = D_MODEL // LANES
RCHUNK = 64
MIX_ROWS = 256
ROUTE_ROWS = 512
MOE_ROWS = 256
VMEM_LIMIT = 56 * 1024 * 1024


def _dot(a, b):
    return jnp.dot(a, b, preferred_element_type=F32)


def _dot_nt(a, b):
    return lax.dot_general(a, b, (((1,), (1,)), ((), ())), preferred_element_type=F32)


def _dot_tn(a, b):
    return lax.dot_general(a, b, (((0,), (0,)), ((), ())), preferred_element_type=F32)


def _layer_norm(z, g, b):
    mu = jnp.mean(z, axis=-1, keepdims=True)
    zc = z - mu
    var = jnp.mean(zc * zc, axis=-1, keepdims=True)
    return zc * lax.rsqrt(var + NORM_EPS) * g + b


def _gelu(z):
    return 0.5 * z * (1.0 + lax.erf(z * (2.0 ** -0.5)))


def _load_rows(ref, n):
    return jnp.concatenate(
        [ref[pl.ds(c, n, stride=ROW_TILES), :] for c in range(ROW_TILES)], axis=1)


def _store_rows(ref, val):
    n = val.shape[0]
    for c in range(ROW_TILES):
        ref[pl.ds(c, n, stride=ROW_TILES), :] = val[:, c * LANES:(c + 1) * LANES]


def _params(n_axes):
    return pltpu.CompilerParams(dimension_semantics=("arbitrary",) * n_axes,
                                vmem_limit_bytes=VMEM_LIMIT)


def _hgrn_kernel(x_ref, s0_ref, win_ref, lb_ref, nw_ref, wout_ref, lg_ref, lbias_ref,
                 o_ref, sout_ref, q_s, lf_s, k_s, v_s, g_s, o_s, st_s, *, cx):
    j = pl.program_id(1)

    @pl.when(j == 0)
    def _():
        for h in range(HEADS):
            st_s[h] = s0_ref[0, h].T

    x = _load_rows(x_ref, cx)
    xb = x.astype(BF16)
    hd = HEADS * DK
    q = _dot(xb, win_ref[:, 0:hd])
    q_s[...] = q * jax.nn.sigmoid(q)
    lb = lb_ref[...]
    f = lb + (1.0 - lb) * jax.nn.sigmoid(_dot(xb, win_ref[:, hd:2 * hd]))
    lf_s[...] = jnp.log(f)
    k_s[...] = 1.0 - f
    v_s[...] = _dot(xb, win_ref[:, 2 * hd:3 * hd])
    g = _dot(xb, win_ref[:, 3 * hd:4 * hd])
    g_s[...] = g * jax.nn.sigmoid(g)

    row = lax.broadcasted_iota(I32, (RCHUNK, DK), 0)
    ti = lax.broadcasted_iota(I32, (RCHUNK, RCHUNK), 0)
    si = lax.broadcasted_iota(I32, (RCHUNK, RCHUNK), 1)
    nw = nw_ref[...]
    levels = (64, 32, 16, 8, 4, 2)
    diag = ti == si
    same_block = {bs: (ti >> (bs.bit_length() - 1)) == (si >> (bs.bit_length() - 1))
                  for bs in levels}
    row_mod = {bs: row & (bs - 1) for bs in levels}

    def head_chunk(rows, h):
        lanes = slice(h * DK, (h + 1) * DK)
        q = q_s[rows, lanes]
        k = k_s[rows, lanes]
        v = v_s[rows, lanes]
        gc = lf_s[rows, lanes]
        for sh in (1, 2, 4, 8, 16, 32):
            gc = gc + jnp.where(row >= sh, pltpu.roll(gc, sh, axis=0), 0.0)
        glast = gc[RCHUNK - 1:RCHUNK, :]
        st = st_s[h]
        vb = v.astype(BF16)
        o = _dot_nt((q * jnp.exp(gc)).astype(BF16), st.astype(BF16))
        kd = (k * jnp.exp(glast - gc)).astype(BF16)
        st_s[h] = st * jnp.exp(glast) + _dot_tn(vb, kd)

        a = jnp.where(diag, _dot_nt(q.astype(BF16), k.astype(BF16)), 0.0)
        rolled = {}

        def shifted(d):
            d = d % RCHUNK
            if d not in rolled:
                rolled[d] = pltpu.roll(gc, d, axis=0)
            return rolled[d]

        for bs in levels:
            half = bs // 2
            rm = row_mod[bs]
            right = rm >= half
            if bs >= 2 * SUBLANES:
                ref = jnp.concatenate(
                    [jnp.broadcast_to(gc[b * bs + half - 1:b * bs + half, :], (bs, DK))
                     for b in range(RCHUNK // bs)], axis=0)
            else:
                ref = gc
                for d in range(1, half + 1):
                    ref = jnp.where(rm == half - 1 + d, shifted(d), ref)
                for d in range(1, half):
                    ref = jnp.where(rm == half - 1 - d, shifted(-d), ref)
            fac = jnp.exp(jnp.where(right, gc - ref, ref - gc))
            ql = jnp.where(right, q * fac, 0.0).astype(BF16)
            kl = jnp.where(right, 0.0, k * fac).astype(BF16)
            a = a + jnp.where(same_block[bs], _dot_nt(ql, kl), 0.0)
        o = o + _dot(a.astype(BF16), vb)
        o = o * lax.rsqrt(jnp.mean(o * o, axis=-1, keepdims=True) + NORM_EPS) * nw
        o_s[rows, lanes] = (o * g_s[rows, lanes]).astype(BF16)

    def chunk(c, carry):
        rows = pl.ds(pl.multiple_of(c * RCHUNK, RCHUNK), RCHUNK)
        for h in range(HEADS):
            head_chunk(rows, h)
        return carry

    lax.fori_loop(0, cx // RCHUNK, chunk, 0)

    y = _dot(o_s[...], wout_ref[...])
    _store_rows(o_ref, _layer_norm(ALPHA * x + y, lg_ref[...], lbias_ref[...]))

    @pl.when(j == pl.num_programs(1) - 1)
    def _():
        for h in range(HEADS):
            sout_ref[0, h] = st_s[h].T


def _hgrn_layer(x_rt, row0, bsz, length, s0, w_in, lb, norm_w, w_out, ln_g, ln_b):
    cx = min(length, MIX_ROWS)
    assert length % cx == 0 and cx % RCHUNK == 0 and row0 % cx == 0
    nj = length // cx
    blk0 = row0 // cx
    const = lambda b, j: (0, 0)
    in_specs = [
        pl.BlockSpec((cx * ROW_TILES, LANES), lambda b, j: (blk0 + b * nj + j, 0)),
        pl.BlockSpec((1, HEADS, DK, DV), lambda b, j: (b, 0, 0, 0)),
        pl.BlockSpec(w_in.shape, const),
        pl.BlockSpec((1, HEADS * DK), const),
        pl.BlockSpec((1, DV), const),
        pl.BlockSpec(w_out.shape, const),
        pl.BlockSpec((1, D_MODEL), const),
        pl.BlockSpec((1, D_MODEL), const),
    ]
    return pl.pallas_call(
        functools.partial(_hgrn_kernel, cx=cx),
        grid=(bsz, nj),
        in_specs=in_specs,
        out_specs=[
            pl.BlockSpec((cx * ROW_TILES, LANES), lambda b, j: (blk0 + b * nj + j, 0)),
            pl.BlockSpec((1, HEADS, DK, DV), lambda b, j: (b, 0, 0, 0)),
        ],
        out_shape=[
            jax.ShapeDtypeStruct(x_rt.shape, F32),
            jax.ShapeDtypeStruct((bsz, HEADS, DK, DV), F32),
        ],
        scratch_shapes=[pltpu.VMEM((cx, HEADS * DK), F32)] * 5
        + [pltpu.VMEM((cx, HEADS * DV), BF16), pltpu.VMEM((HEADS, DV, DK), F32)],
        input_output_aliases={0: 0},
        compiler_params=_params(2),
        name="hgrn_layer",
    )(x_rt, s0, w_in, lb, norm_w, w_out, ln_g, ln_b)


def _sgu_kernel(*refs, cx, chunk, with_v):
    (x_ref, win_ref, vg_ref, vb_ref, ws_ref, bs_ref, wout_ref, lg_ref, lbias_ref) = refs[:9]
    o_ref = refs[9]
    v_ref = refs[10] if with_v else None
    x = _load_rows(x_ref, cx)
    xb = x.astype(BF16)
    u = _gelu(_dot(xb, win_ref[:, 0:SGU_WIDTH]))
    v = _gelu(_dot(xb, win_ref[:, SGU_WIDTH:2 * SGU_WIDTH]))
    v = _layer_norm(v, vg_ref[...], vb_ref[...])
    if with_v:
        v_ref[...] = v
    vbf = v.astype(BF16)
    ti = lax.broadcasted_iota(I32, (chunk, chunk), 0)
    si = lax.broadcasted_iota(I32, (chunk, chunk), 1)
    cols = []
    for g in range(SGU_GROUPS):
        wc = jnp.where(ti >= si, ws_ref[g, 0:chunk, 0:chunk], 0.0).astype(BF16)
        bias = bs_ref[g, 0:chunk, :]
        lanes = slice(g * SGU_GROUP_DIM, (g + 1) * SGU_GROUP_DIM)
        cols.append(jnp.concatenate(
            [_dot(wc, vbf[c * chunk:(c + 1) * chunk, lanes]) + bias for c in range(cx // chunk)],
            axis=0))
    mixed = jnp.concatenate(cols, axis=1)
    y = _dot((u * mixed).astype(BF16), wout_ref[...])
    _store_rows(o_ref, _layer_norm(ALPHA * x + y, lg_ref[...], lbias_ref[...]))


def _sgu_layer(x_rt, row0, bsz, length, w_in, vg, vb, w_s, b_s, w_out, ln_g, ln_b, with_v):
    chunk = min(length, SGU_CHUNK)
    cx = min(length, MIX_ROWS)
    assert length % cx == 0 and cx % chunk == 0 and row0 % cx == 0
    n_rows = bsz * length
    blk0 = row0 // cx
    const2 = lambda i: (0, 0)
    const3 = lambda i: (0, 0, 0)
    in_specs = [
        pl.BlockSpec((cx * ROW_TILES, LANES), lambda i: (blk0 + i, 0)),
        pl.BlockSpec(w_in.shape, const2),
        pl.BlockSpec((1, SGU_WIDTH), const2),
        pl.BlockSpec((1, SGU_WIDTH), const2),
        pl.BlockSpec(w_s.shape, const3),
        pl.BlockSpec(b_s.shape, const3),
        pl.BlockSpec(w_out.shape, const2),
        pl.BlockSpec((1, D_MODEL), const2),
        pl.BlockSpec((1, D_MODEL), const2),
    ]
    out_specs = [pl.BlockSpec((cx * ROW_TILES, LANES), lambda i: (blk0 + i, 0))]
    out_shape = [jax.ShapeDtypeStruct(x_rt.shape, F32)]
    if with_v:
        out_specs.append(pl.BlockSpec((cx, SGU_WIDTH), lambda i: (i, 0)))
        out_shape.append(jax.ShapeDtypeStruct((n_rows, SGU_WIDTH), F32))
    return pl.pallas_call(
        functools.partial(_sgu_kernel, cx=cx, chunk=chunk, with_v=with_v),
        grid=(n_rows // cx,),
        in_specs=in_specs,
        out_specs=out_specs,
        out_shape=out_shape,
        input_output_aliases={0: 0},
        compiler_params=_params(1),
        name="sgu_layer",
    )(x_rt, w_in, vg, vb, w_s, b_s, w_out, ln_g, ln_b)


def _router_kernel(x_ref, wt_ref, b_ref, idx_ref, gate_ref, rank_ref, cnt_ref, carry_s):
    i = pl.program_id(0)

    @pl.when(i == 0)
    def _():
        carry_s[...] = jnp.zeros_like(carry_s)

    tb = idx_ref.shape[1]
    logits = lax.dot_general(wt_ref[...], _load_rows(x_ref, tb), (((1,), (1,)), ((), ())),
                             preferred_element_type=F32,
                             precision=lax.Precision.HIGHEST) + b_ref[...]
    erow = lax.broadcasted_iota(I32, (N_EXPERTS, tb), 0)
    work = logits
    tops, hots = [], []
    for kk in range(TOP_K):
        m = jnp.max(work, axis=0, keepdims=True)
        idx = jnp.min(jnp.where(work == m, erow, N_EXPERTS), axis=0, keepdims=True)
        hot = erow == idx
        idx_ref[kk:kk + 1, :] = idx
        tops.append(m)
        hots.append(hot)
        work = jnp.where(hot, -jnp.inf, work)
    exps = [jnp.exp(m - tops[0]) for m in tops]
    denom = exps[0] + exps[1] + exps[2] + exps[3]
    for kk in range(TOP_K):
        gate_ref[kk:kk + 1, :] = exps[kk] / denom
    sel = jnp.zeros((N_EXPERTS, tb), F32)
    for hot in hots:
        sel = sel + hot.astype(F32)
    t0 = lax.broadcasted_iota(I32, (tb, tb), 0)
    t1 = lax.broadcasted_iota(I32, (tb, tb), 1)
    before = (t0 < t1).astype(BF16)
    prefix = _dot(sel.astype(BF16), before) + carry_s[...]
    for kk in range(TOP_K):
        r = jnp.sum(jnp.where(hots[kk], prefix, 0.0), axis=0, keepdims=True)
        rank_ref[kk:kk + 1, :] = r.astype(I32)
    carry_s[...] = carry_s[...] + jnp.sum(sel, axis=1, keepdims=True)

    @pl.when(i == pl.num_programs(0) - 1)
    def _():
        cnt_ref[...] = carry_s[...].astype(I32)


def _router(x_rt, w_router_t, b_router):
    t_all = x_rt.shape[0] // ROW_TILES
    tb = ROUTE_ROWS
    assert t_all % tb == 0
    const = lambda i: (0, 0)
    return pl.pallas_call(
        _router_kernel,
        grid=(t_all // tb,),
        in_specs=[
            pl.BlockSpec((tb * ROW_TILES, LANES), lambda i: (i, 0)),
            pl.BlockSpec((N_EXPERTS, D_MODEL), const),
            pl.BlockSpec((N_EXPERTS, 1), const),
        ],
        out_specs=[
            pl.BlockSpec((TOP_K, tb), lambda i: (0, i)),
            pl.BlockSpec((TOP_K, tb), lambda i: (0, i)),
            pl.BlockSpec((TOP_K, tb), lambda i: (0, i)),
            pl.BlockSpec((N_EXPERTS, 1), const),
        ],
        out_shape=[
            jax.ShapeDtypeStruct((TOP_K, t_all), I32),
            jax.ShapeDtypeStruct((TOP_K, t_all), F32),
            jax.ShapeDtypeStruct((TOP_K, t_all), I32),
            jax.ShapeDtypeStruct((N_EXPERTS, 1), I32),
        ],
        scratch_shapes=[pltpu.VMEM((N_EXPERTS, 1), F32)],
        compiler_params=_params(1),
        name="moe_router",
    )(x_rt, w_router_t, b_router)


def _row_copy(src, src_row, dst, dst_row, sem):
    return pltpu.make_async_copy(
        src.at[pl.ds(pl.multiple_of(src_row * ROW_TILES, ROW_TILES), ROW_TILES)],
        dst.at[pl.ds(pl.multiple_of(dst_row * ROW_TILES, ROW_TILES), ROW_TILES)],
        sem)


def _expert_kernel(be_ref, nused_ref, tok_cur, tok_next, dst_prev, dst_cur,
                   x_ref, y_init_ref, wup_ref, bup_ref, wdn_ref, bdn_ref, y_ref, xg, yo, gsem, ssem):
    del be_ref, y_init_ref
    i = pl.program_id(0)
    nused = nused_ref[0]
    tm = tok_cur.shape[2]
    slot = lax.rem(i, 2)
    other = 1 - slot

    def issue_gather(tok_ref, s):
        for r in range(tm):
            _row_copy(x_ref, tok_ref[0, 0, r], xg.at[s], r, gsem.at[s]).start()

    def issue_scatter(dst_ref, s):
        for r in range(tm):
            _row_copy(yo.at[s], r, y_ref, dst_ref[0, 0, r], ssem.at[s]).start()

    def wait_gather(s):
        pltpu.make_async_copy(xg.at[s], xg.at[s], gsem.at[s]).wait()

    def wait_scatter(s):
        pltpu.make_async_copy(yo.at[s], yo.at[s], ssem.at[s]).wait()

    def compute(s):
        xb = _load_rows(xg.at[s], tm).astype(BF16)
        h = _dot(xb, wup_ref[0]) + bup_ref[0]
        glu = jnp.minimum(h[:, 0:D_FF], SWIGLU_LIMIT)
        lin = jnp.clip(h[:, D_FF:2 * D_FF], -SWIGLU_LIMIT, SWIGLU_LIMIT)
        act = glu * jax.nn.sigmoid(SWIGLU_ALPHA * glu) * (lin + 1.0)
        _store_rows(yo.at[s], _dot(act.astype(BF16), wdn_ref[0]) + bdn_ref[0])

    @pl.when(i == 0)
    def _():
        issue_gather(tok_cur, 0)
        wait_gather(0)
        issue_gather(tok_next, 1)
        compute(0)

    @pl.when(jnp.logical_and(i > 0, i < nused))
    def _():
        wait_gather(slot)

        @pl.when(i >= 2)
        def _():
            wait_scatter(slot)

        issue_gather(tok_next, other)
        issue_scatter(dst_prev, other)
        compute(slot)

    @pl.when(i == nused - 1)
    def _():
        wait_gather(other)

        @pl.when(i >= 1)
        def _():
            wait_scatter(other)

        issue_scatter(dst_cur, slot)
        wait_scatter(slot)


def _experts(x_rt, src_tok, dst_row, block_e, nused, w_up, b_up, w_down, b_down, n_out_rows):
    tm = MOE_ROWS
    n_blocks = src_tok.shape[0]
    smem = lambda f: pl.BlockSpec((1, 1, tm), f, memory_space=pltpu.SMEM)
    return pl.pallas_call(
        _expert_kernel,
        grid_spec=pltpu.PrefetchScalarGridSpec(
            num_scalar_prefetch=2,
            grid=(n_blocks,),
            in_specs=[
                smem(lambda i, be, nu: (i, 0, 0)),
                smem(lambda i, be, nu: (jnp.minimum(i + 1, n_blocks - 1), 0, 0)),
                smem(lambda i, be, nu: (jnp.maximum(i - 1, 0), 0, 0)),
                smem(lambda i, be, nu: (i, 0, 0)),
                pl.BlockSpec(memory_space=pl.ANY),
                pl.BlockSpec(memory_space=pl.ANY),
                pl.BlockSpec((1, D_MODEL, 2 * D_FF), lambda i, be, nu: (be[i], 0, 0)),
                pl.BlockSpec((1, 1, 2 * D_FF), lambda i, be, nu: (be[i], 0, 0)),
                pl.BlockSpec((1, D_FF, D_MODEL), lambda i, be, nu: (be[i], 0, 0)),
                pl.BlockSpec((1, 1, D_MODEL), lambda i, be, nu: (be[i], 0, 0)),
            ],
            out_specs=pl.BlockSpec(memory_space=pl.ANY),
            scratch_shapes=[
                pltpu.VMEM((2, tm * ROW_TILES, LANES), F32),
                pltpu.VMEM((2, tm * ROW_TILES, LANES), F32),
                pltpu.SemaphoreType.DMA((2,)),
                pltpu.SemaphoreType.DMA((2,)),
            ],
        ),
        out_shape=jax.ShapeDtypeStruct((n_out_rows * ROW_TILES, LANES), F32),
        input_output_aliases={7: 0},
        compiler_params=_params(1),
        name="moe_experts",
    )(block_e, nused, src_tok, src_tok, dst_row, dst_row, x_rt,
      jnp.zeros((n_out_rows * ROW_TILES, LANES), F32), w_up, b_up, w_down, b_down)


def _combine_kernel(x_ref, y0_ref, y1_ref, y2_ref, y3_ref, gate_ref, lg_ref, lbias_ref, o_ref,
                    *, row_tile_out):
    tb = gate_ref.shape[0]
    gates = gate_ref[...]
    z = ALPHA * _load_rows(x_ref, tb)
    for kk, y_ref in enumerate((y0_ref, y1_ref, y2_ref, y3_ref)):
        z = z + _load_rows(y_ref, tb) * gates[:, kk:kk + 1]
    out = _layer_norm(z, lg_ref[...], lbias_ref[...])
    if row_tile_out:
        _store_rows(o_ref, out)
    else:
        o_ref[...] = out


def _combine(x_rt, ybuf, gates_t, ln_g, ln_b, row0, n_rows, row_tile_out):
    t_all = x_rt.shape[0] // ROW_TILES
    tb = MOE_ROWS
    assert t_all % tb == 0 and row0 % tb == 0 and n_rows % tb == 0
    blk0 = row0 // tb
    per_k = t_all // tb
    const = lambda i: (0, 0)
    tile_spec = lambda f: pl.BlockSpec((tb * ROW_TILES, LANES), f)
    if row_tile_out:
        out_spec = tile_spec(lambda i: (blk0 + i, 0))
        out_shape = jax.ShapeDtypeStruct((n_rows * ROW_TILES, LANES), F32)
    else:
        out_spec = pl.BlockSpec((tb, D_MODEL), lambda i: (i, 0))
        out_shape = jax.ShapeDtypeStruct((n_rows, D_MODEL), F32)
    return pl.pallas_call(
        functools.partial(_combine_kernel, row_tile_out=row_tile_out),
        grid=(n_rows // tb,),
        in_specs=[
            tile_spec(lambda i: (blk0 + i, 0)),
            tile_spec(lambda i: (0 * per_k + blk0 + i, 0)),
            tile_spec(lambda i: (1 * per_k + blk0 + i, 0)),
            tile_spec(lambda i: (2 * per_k + blk0 + i, 0)),
            tile_spec(lambda i: (3 * per_k + blk0 + i, 0)),
            pl.BlockSpec((tb, TOP_K), lambda i: (blk0 + i, 0)),
            pl.BlockSpec((1, D_MODEL), const),
            pl.BlockSpec((1, D_MODEL), const),
        ],
        out_specs=out_spec,
        out_shape=out_shape,
        compiler_params=_params(1),
        name="moe_combine",
    )(x_rt, ybuf, ybuf, ybuf, ybuf, gates_t, ln_g, ln_b)


def _moe_experts_layer(x_rt, w_router, b_router, w_up, b_up, w_down, b_down):
    t_all = x_rt.shape[0] // ROW_TILES
    tm = MOE_ROWS
    idx, gates, rank, counts = _router(x_rt, w_router.T, b_router.reshape(N_EXPERTS, 1))
    counts = counts.reshape(N_EXPERTS)
    padded = (counts + tm - 1) // tm * tm
    pad_end = jnp.cumsum(padded)
    pad_start = pad_end - padded
    hot = idx[:, :, None] == jnp.arange(N_EXPERTS, dtype=I32)[None, None, :]
    dest = jnp.sum(jnp.where(hot, pad_start[None, None, :], 0), axis=-1) + rank
    n_blocks = -(-(t_all * TOP_K + N_EXPERTS * (tm - 1)) // tm)
    n_grouped = n_blocks * tm
    n_slots = TOP_K * t_all
    block_e = jnp.minimum(
        jnp.sum(pad_end[None, :] <= (jnp.arange(n_blocks, dtype=I32) * tm)[:, None], axis=1),
        N_EXPERTS - 1).astype(I32)
    nused = (pad_end[-1:] // tm).astype(I32)
    slot_id = jnp.arange(n_slots, dtype=I32).reshape(TOP_K, t_all)
    dst_row = jnp.full((n_grouped,), -1, I32).at[dest.reshape(-1)].set(
        slot_id.reshape(-1), unique_indices=True)
    is_pad = dst_row < 0
    dst_row = jnp.where(is_pad, n_slots + jnp.cumsum(is_pad.astype(I32)) - 1, dst_row)
    src_tok = jnp.where(is_pad, t_all - 1, dst_row % t_all)
    ybuf = _experts(x_rt, src_tok.reshape(n_blocks, 1, tm), dst_row.reshape(n_blocks, 1, tm),
                    block_e, nused, w_up, b_up.reshape(N_EXPERTS, 1, 2 * D_FF),
                    w_down, b_down.reshape(N_EXPERTS, 1, D_MODEL), n_grouped)
    return ybuf, gates.T


def kernel(x_prompt, x_sample, state_hgrn, hgrn_w_in, hgrn_lb_logits, hgrn_norm_w, hgrn_w_out,
           sgu_w_in, sgu_ln_g, sgu_ln_b, sgu_w_s, sgu_b_s, sgu_w_out,
           ln_mix_g, ln_mix_b, ln_ffn_g, ln_ffn_b,
           moe_w_router, moe_b_router, moe_w_up, moe_b_up, moe_w_down, moe_b_down):
    bp, lp, _ = x_prompt.shape
    bs, ls, _ = x_sample.shape
    tp, ts = bp * lp, bs * ls
    t_all = tp + ts
    x = jnp.concatenate([x_prompt.reshape(tp, D_MODEL), x_sample.reshape(ts, D_MODEL)], axis=0)
    x = x.reshape(t_all * ROW_TILES, LANES)
    lb_all = jnp.cumsum(jax.nn.softmax(hgrn_lb_logits.astype(F32), axis=0), axis=0)
    row = lambda a: a.reshape(1, -1)

    w_in = hgrn_w_in[0].astype(BF16)
    w_out = hgrn_w_out[0].astype(BF16)
    common = (w_in, row(lb_all[0]), row(hgrn_norm_w[0]), w_out, row(ln_mix_g[0]), row(ln_mix_b[0]))
    zero_state = jnp.zeros((bp, HEADS, DK, DV), F32)
    x1, st_prompt = _hgrn_layer(x, 0, bp, lp, zero_state, *common)
    x1, st_sample = _hgrn_layer(x1, tp, bs, ls, state_hgrn[0], *common)
    ybuf, gates = _moe_experts_layer(x1, moe_w_router[0], moe_b_router[0],
                                     moe_w_up[0].astype(BF16), moe_b_up[0],
                                     moe_w_down[0].astype(BF16), moe_b_down[0])
    x2 = _combine(x1, ybuf, gates, row(ln_ffn_g[0]), row(ln_ffn_b[0]), 0, t_all, True)

    common = (sgu_w_in[0].astype(BF16), row(sgu_ln_g[0]), row(sgu_ln_b[0]), sgu_w_s[0],
              sgu_b_s[0].reshape(SGU_GROUPS, SGU_CHUNK, 1), sgu_w_out[0].astype(BF16),
              row(ln_mix_g[1]), row(ln_mix_b[1]))
    (x3,) = _sgu_layer(x2, 0, bp, lp, *common, with_v=False)
    x3, v_sample = _sgu_layer(x3, tp, bs, ls, *common, with_v=True)
    ybuf, gates = _moe_experts_layer(x3, moe_w_router[1], moe_b_router[1],
                                     moe_w_up[1].astype(BF16), moe_b_up[1],
                                     moe_w_down[1].astype(BF16), moe_b_down[1])
    lg, lbias = row(ln_ffn_g[1]), row(ln_ffn_b[1])
    y_prompt = _combine(x3, ybuf, gates, lg, lbias, 0, tp, False)
    y_sample = _combine(x3, ybuf, gates, lg, lbias, tp, ts, False)

    return (y_prompt.reshape(bp, lp, D_MODEL), y_sample.reshape(bs, ls, D_MODEL),
            st_prompt[None], st_sample[None], v_sample.reshape(1, bs, ls, SGU_WIDTH))
```
